```python
import jax
import jax.numpy as jnp
from jax import lax
import numpy as np

D_MODEL = 4096
BATCH = 1
SEQ = 16384
DEPTH = 4

EPS = 1e-5
BLOCK = 128
SB_SUPER = 2048
SB_HEAD_DIM = 256
SB_HEADS = D_MODEL // (2 * SB_HEAD_DIM)
SB_WIDTH = SB_HEADS * SB_HEAD_DIM
CHUNK = 128
SG_GROUP_DIM = 128
SG_GROUPS = D_MODEL // (2 * SG_GROUP_DIM)
SG_WIDTH = SG_GROUPS * SG_GROUP_DIM
EV_IN_COLS = 3 * SB_WIDTH + 2 * SG_WIDTH
EV_OUT_COLS = SB_WIDTH + SG_WIDTH
SWA_HEAD_DIM = 64
SWA_HEADS = D_MODEL // SWA_HEAD_DIM
SWA_KV_HEADS = SWA_HEADS // 8
SWA_GROUP = SWA_HEADS // SWA_KV_HEADS
SWA_Q_WIDTH = SWA_HEADS * SWA_HEAD_DIM
SWA_KV_WIDTH = SWA_KV_HEADS * SWA_HEAD_DIM
SWA_IN_COLS = SWA_Q_WIDTH + 2 * SWA_KV_WIDTH
WINDOW = 128
D_FF = 6144
CONV_WIDTH = 3
N_EVEN = (DEPTH + 1) // 2
N_ODD = DEPTH // 2

kernel_name = 'hybrid_stickbreak_sgmlp_swa_convffn'


def rms_norm(x, g):
    xf = x.astype(jnp.float32)
    y = xf * lax.rsqrt(jnp.mean(xf * xf, axis=-1, keepdims=True) + EPS)
    return (y * g.astype(jnp.float32)).astype(x.dtype)


def stick_breaking_attention(q, k, v):
    B, S, H, Dh = q.shape
    q = q * (Dh ** -0.5)
    sup = min(S, SB_SUPER)
    tri_incl = jnp.tril(jnp.ones((BLOCK, BLOCK), jnp.float32))
    outs = []
    for start in range(0, S, sup):
        end = min(S, start + sup)
        nkb = end // BLOCK
        nq = (end - start) // BLOCK
        kk = k[:, :end].reshape(B, nkb, BLOCK, H, Dh)
        vv = v[:, :end].reshape(B, nkb, BLOCK, H, Dh)
        qb = q[:, start:end].reshape(B, nq, BLOCK, H, Dh).transpose(1, 0, 3, 2, 4)
        k_pos = jnp.arange(end).reshape(nkb, BLOCK)
        strict = jnp.tril(jnp.ones((nkb, nkb), jnp.float32), -1)

        def block(args, kk=kk, vv=vv, k_pos=k_pos, strict=strict):
            i, qi = args
            z = jnp.einsum('bhqd,bnkhd->bhqnk', qi, kk, preferred_element_type=jnp.float32)
            q_pos = i * BLOCK + jnp.arange(BLOCK)
            causal = k_pos[None] < q_pos[:, None, None]
            log_not = jnp.where(causal, -jax.nn.softplus(z), 0.0)
            within = jnp.einsum('bhqnk,kj->bhqnj', log_not, tri_incl, precision=lax.Precision.HIGHEST)
            later = jnp.einsum('bhqn,nm->bhqm', log_not.sum(-1), strict, precision=lax.Precision.HIGHEST)
            a = jnp.where(causal, jnp.exp(z + within + later[..., None]), 0.0)
            return jnp.einsum('bhqnk,bnkhd->bqhd', a.astype(vv.dtype), vv)

        out = lax.map(block, (start // BLOCK + jnp.arange(nq), qb))
        outs.append(out.transpose(1, 0, 2, 3, 4).reshape(B, end - start, H * Dh))
    return jnp.concatenate(outs, axis=1)


def chunked_spatial_gating(z, ln_g, ln_b, w_s, b_s):
    B, S, _ = z.shape
    u, v = jnp.split(z, 2, axis=-1)
    v = v.reshape(B, S // CHUNK, CHUNK, SG_GROUPS, SG_GROUP_DIM)
    vf = v.astype(jnp.float32)
    mu = jnp.mean(vf, axis=-1, keepdims=True)
    var = jnp.mean(jnp.square(vf - mu), axis=-1, keepdims=True)
    vn = ((vf - mu) * lax.rsqrt(var + EPS) * ln_g + ln_b).astype(z.dtype)
    tri = jnp.tril(jnp.ones((CHUNK, CHUNK), dtype=bool))
    w = jnp.where(tri[None], w_s, 0.0).astype(z.dtype)
    mixed = jnp.einsum('gts,bcsgd->bctgd', w, vn) + b_s.T.astype(z.dtype)[None, None, :, :, None]
    return u * mixed.reshape(B, S, SG_WIDTH)


def even_mixer(h, w_in, ln_g, ln_b, w_s, b_s, w_out):
    B, S, _ = h.shape
    proj = h @ w_in
    q, k, v, z = jnp.split(proj, [SB_WIDTH, 2 * SB_WIDTH, 3 * SB_WIDTH], axis=-1)
    shp = (B, S, SB_HEADS, SB_HEAD_DIM)
    a_out = stick_breaking_attention(q.reshape(shp), k.reshape(shp), v.reshape(shp))
    b_out = chunked_spatial_gating(jax.nn.gelu(z), ln_g, ln_b, w_s, b_s)
    return jnp.concatenate([a_out, b_out], axis=-1) @ w_out


def sliding_window_attention(q, k, v, sinks):
    B, S, HQ, Dh = q.shape
    nb = S // BLOCK
    scale = Dh ** -0.5
    qb = q.reshape(B, nb, BLOCK, SWA_KV_HEADS, SWA_GROUP, Dh).transpose(1, 0, 2, 3, 4, 5)
    pad = jnp.zeros((B, BLOCK, SWA_KV_HEADS, Dh), k.dtype)
    kp = jnp.concatenate([pad, k], axis=1)
    vp = jnp.concatenate([pad, v], axis=1)
    diff = jnp.arange(BLOCK)[:, None] + BLOCK - jnp.arange(2 * BLOCK)[None, :]
    in_window = (diff >= 0) & (diff < WINDOW)
    sink = sinks.astype(jnp.float32).reshape(SWA_KV_HEADS, SWA_GROUP)[None, :, :, None, None]

    def block(args):
        i, qi = args
        ki = lax.dynamic_slice_in_dim(kp, i * BLOCK, 2 * BLOCK, axis=1)
        vi = lax.dynamic_slice_in_dim(vp, i * BLOCK, 2 * BLOCK, axis=1)
        s = jnp.einsum('bqhgd,bkhd->bhgqk', qi, ki, preferred_element_type=jnp.float32) * scale
        k_pos = (i - 1) * BLOCK + jnp.arange(2 * BLOCK)
        valid = in_window & (k_pos >= 0)[None, :]
        s = jnp.where(valid, s, -jnp.inf)
        logits = jnp.concatenate([s, jnp.broadcast_to(sink, s.shape[:-1] + (1,))], axis=-1)
        p = jax.nn.softmax(logits, axis=-1)[..., :-1]
        return jnp.einsum('bhgqk,bkhd->bqhgd', p.astype(vi.dtype), vi)

    out = lax.map(block, (jnp.arange(nb), qb))
    return out.transpose(1, 0, 2, 3, 4, 5).reshape(B, S, HQ * Dh)


def odd_mixer(h, w_qkv, b_qkv, sinks, w_out, b_out):
    B, S, _ = h.shape
    proj = h @ w_qkv + b_qkv
    q, k, v = jnp.split(proj, [SWA_Q_WIDTH, SWA_Q_WIDTH + SWA_KV_WIDTH], axis=-1)
    q = q.reshape(B, S, SWA_HEADS, SWA_HEAD_DIM)
    k = k.reshape(B, S, SWA_KV_HEADS, SWA_HEAD_DIM)
    v = v.reshape(B, S, SWA_KV_HEADS, SWA_HEAD_DIM)
    return sliding_window_attention(q, k, v, sinks) @ w_out + b_out


def conv_ffn(h, w_in, conv_w, conv_b, w_down):
    up = h @ w_in
    kern = conv_w[:, None, :].astype(up.dtype)
    up = lax.conv_general_dilated(up, kern, window_strides=(1,), padding=[(CONV_WIDTH - 1, 0)],
                                  dimension_numbers=('NWC', 'WIO', 'NWC'),
                                  feature_group_count=up.shape[-1]) + conv_b
    gate, val = jnp.split(up, 2, axis=-1)
    return (jax.nn.silu(gate) * val) @ w_down


def setup_inputs(seed: int = 0) -> dict:
    key = jax.random.key(seed)
    ks = jax.random.split(key, 20)

    def nrm(k, shape, scale):
        return jax.random.normal(k, shape, jnp.float32) * scale

    def gain(k, shape):
        return 1.0 + 0.01 * jax.random.normal(k, shape, jnp.float32)

    return {
        'x': nrm(ks[0], (BATCH, SEQ, D_MODEL), 1.0),
        'mix_norm': gain(ks[1], (DEPTH, D_MODEL)),
        'ffn_norm': gain(ks[2], (DEPTH, D_MODEL)),
        'final_norm': gain(ks[3], (D_MODEL,)),
        'ev_w_in': nrm(ks[4], (N_EVEN, D_MODEL, EV_IN_COLS), D_MODEL ** -0.5),
        'ev_sg_ln_g': gain(ks[5], (N_EVEN, SG_GROUPS, SG_GROUP_DIM)),
        'ev_sg_ln_b': nrm(ks[6], (N_EVEN, SG_GROUPS, SG_GROUP_DIM), 0.01),
        'ev_sg_w': nrm(ks[7], (N_EVEN, SG_GROUPS, CHUNK, CHUNK), CHUNK ** -0.5),
        'ev_sg_b': gain(ks[8], (N_EVEN, SG_GROUPS, CHUNK)),
        'ev_w_out': nrm(ks[9], (N_EVEN, EV_OUT_COLS, D_MODEL), EV_OUT_COLS ** -0.5),
        'od_w_qkv': nrm(ks[10], (N_ODD, D_MODEL, SWA_IN_COLS), D_MODEL ** -0.5),
        'od_b_qkv': nrm(ks[11], (N_ODD, SWA_IN_COLS), 0.01),
        'od_sinks': nrm(ks[12], (N_ODD, SWA_HEADS), 0.5),
        'od_w_out': nrm(ks[13], (N_ODD, SWA_Q_WIDTH, D_MODEL), SWA_Q_WIDTH ** -0.5),
        'od_b_out': nrm(ks[14], (N_ODD, D_MODEL), 0.01),
        'ffn_w_in': nrm(ks[15], (DEPTH, D_MODEL, 2 * D_FF), D_MODEL ** -0.5),
        'ffn_conv_w': nrm(ks[16], (DEPTH, CONV_WIDTH, 2 * D_FF), CONV_WIDTH ** -0.5),
        'ffn_conv_b': nrm(ks[17], (DEPTH, 2 * D_FF), 0.01),
        'ffn_w_down': nrm(ks[18], (DEPTH, D_FF, D_MODEL), D_FF ** -0.5),
    }


def reference(x, mix_norm, ffn_norm, final_norm, ev_w_in, ev_sg_ln_g, ev_sg_ln_b, ev_sg_w,
              ev_sg_b, ev_w_out, od_w_qkv, od_b_qkv, od_sinks, od_w_out, od_b_out,
              ffn_w_in, ffn_conv_w, ffn_conv_b, ffn_w_down):
    for layer in range(DEPTH):
        h = rms_norm(x, mix_norm[layer])
        if layer % 2 == 0:
            e = layer // 2
            x = x + even_mixer(h, ev_w_in[e], ev_sg_ln_g[e], ev_sg_ln_b[e], ev_sg_w[e],
                               ev_sg_b[e], ev_w_out[e])
        else:
            o = layer // 2
            x = x + odd_mixer(h, od_w_qkv[o], od_b_qkv[o], od_sinks[o], od_w_out[o], od_b_out[o])
        x = x + conv_ffn(rms_norm(x, ffn_norm[layer]), ffn_w_in[layer], ffn_conv_w[layer],
                         ffn_conv_b[layer], ffn_w_down[layer])
    return rms_norm(x, final_norm)
```

```python
import functools

import jax
import jax.numpy as jnp
from jax import lax
from jax.experimental import pallas as pl
from jax.experimental.pallas import tpu as pltpu

F32 = jnp.float32
BF16 = jnp.bfloat16

D_MODEL = 4096
SEQ = 16384
DEPTH = 4
EPS = 1e-5
BLOCK = 128
SB_HEAD_DIM = 256
SB_HEADS = 8
SB_WIDTH = SB_HEADS * SB_HEAD_DIM
CHUNK = 128
SG_GROUP_DIM = 128
SG_GROUPS = 16
SG_WIDTH = SG_GROUPS * SG_GROUP_DIM
SWA_HEAD_DIM = 64
SWA_HEADS = 64
SWA_KV_HEADS = 8
SWA_GROUP = SWA_HEADS // SWA_KV_HEADS
SWA_Q_WIDTH = SWA_HEADS * SWA_HEAD_DIM
SWA_KV_WIDTH = SWA_KV_HEADS * SWA_HEAD_DIM
WINDOW = 128
D_FF = 6144
CONV_WIDTH = 3

V7X_VMEM_LIMIT_BYTES = 56 * 1024 * 1024
SUBLANES = 8


def _params(n_axes):
    return pltpu.CompilerParams(
        dimension_semantics=("arbitrary",) * n_axes,
        vmem_limit_bytes=V7X_VMEM_LIMIT_BYTES,
    )


def _rms_norm_rows(x, g):
    ms = jnp.mean(x * x, axis=-1, keepdims=True)
    return x * lax.rsqrt(ms + EPS) * g


def _gelu_tanh(x):
    c = 0.7978845608028654
    return 0.5 * x * (1.0 + jnp.tanh(c * (x + 0.044715 * (x * x * x))))


def _norm_mm_kernel(x_ref, g_ref, w_ref, *rest, has_bias, act):
    if has_bias:
        b_ref, o_ref, h_ref = rest
    else:
        o_ref, h_ref = rest

    @pl.when(pl.program_id(1) == 0)
    def _():
        h_ref[...] = _rms_norm_rows(x_ref[...], g_ref[...]).astype(BF16)

    acc = jnp.dot(h_ref[...], w_ref[...], preferred_element_type=F32)
    if has_bias:
        acc = acc + b_ref[...]
    if act == "gelu":
        acc = _gelu_tanh(acc)
    o_ref[...] = acc.astype(o_ref.dtype)


def _norm_mm(x, g, w, bias=None, act=None, out_dtype=BF16, tm=512, tn=1024):
    m, d = x.shape
    n = w.shape[1]
    tn = min(tn, n)
    assert m % tm == 0 and n % tn == 0
    in_specs = [
        pl.BlockSpec((tm, d), lambda i, j: (i, 0)),
        pl.BlockSpec((1, d), lambda i, j: (0, 0)),
        pl.BlockSpec((d, tn), lambda i, j: (0, j)),
    ]
    args = [x, g.reshape(1, d), w]
    if bias is not None:
        in_specs.append(pl.BlockSpec((1, tn), lambda i, j: (0, j)))
        args.append(bias.reshape(1, n))
    return pl.pallas_call(
        functools.partial(_norm_mm_kernel, has_bias=bias is not None, act=act),
        grid=(m // tm, n // tn),
        in_specs=in_specs,
        out_specs=pl.BlockSpec((tm, tn), lambda i, j: (i, j)),
        out_shape=jax.ShapeDtypeStruct((m, n), out_dtype),
        scratch_shapes=[pltpu.VMEM((tm, d), BF16)],
        compiler_params=_params(2),
        name="norm_mm",
    )(*args)


def _mm_res_kernel(a_ref, w_ref, x_ref, *rest, has_bias):
    if has_bias:
        b_ref, o_ref = rest
    else:
        (o_ref,) = rest
    acc = jnp.dot(a_ref[...], w_ref[...], preferred_element_type=F32)
    if has_bias:
        acc = acc + b_ref[...]
    o_ref[...] = x_ref[...] + acc


def _mm_res(a, w, x, bias=None, tm=512):
    m, k = a.shape
    n = w.shape[1]
    tn = 1024 if k <= 4096 else 512
    assert m % tm == 0 and n % tn == 0
    in_specs = [
        pl.BlockSpec((tm, k), lambda i, j: (i, 0)),
        pl.BlockSpec((k, tn), lambda i, j: (0, j)),
        pl.BlockSpec((tm, tn), lambda i, j: (i, j)),
    ]
    args = [a, w, x]
    if bias is not None:
        in_specs.append(pl.BlockSpec((1, tn), lambda i, j: (0, j)))
        args.append(bias.reshape(1, n))
    return pl.pallas_call(
        functools.partial(_mm_res_kernel, has_bias=bias is not None),
        grid=(m // tm, n // tn),
        in_specs=in_specs,
        out_specs=pl.BlockSpec((tm, tn), lambda i, j: (i, j)),
        out_shape=jax.ShapeDtypeStruct((m, n), F32),
        compiler_params=_params(2),
        name="mm_res",
    )(*args)


def _ffn_in_kernel(x_ref, g_ref, wg_ref, wv_ref, cwg_ref, cwv_ref, cbg_ref, cbv_ref,
                   o_ref, h_ref, ug_ref, uv_ref, carry_ref, *, tm, tn):
    i = pl.program_id(0)
    j = pl.program_id(1)

    @pl.when(j == 0)
    def _():
        h_ref[...] = _rms_norm_rows(x_ref[...], g_ref[...]).astype(BF16)

    @pl.when(i == 0)
    def _():
        ug_ref[0:SUBLANES, :] = jnp.zeros((SUBLANES, tn), F32)
        uv_ref[0:SUBLANES, :] = jnp.zeros((SUBLANES, tn), F32)

    @pl.when(i > 0)
    def _():
        ug_ref[0:SUBLANES, :] = carry_ref[j, :, 0:tn]
        uv_ref[0:SUBLANES, :] = carry_ref[j, :, tn:2 * tn]

    h = h_ref[...]
    ug_ref[SUBLANES:SUBLANES + tm, :] = jnp.dot(h, wg_ref[...], preferred_element_type=F32)
    uv_ref[SUBLANES:SUBLANES + tm, :] = jnp.dot(h, wv_ref[...], preferred_element_type=F32)
    carry_ref[j, :, 0:tn] = ug_ref[tm:tm + SUBLANES, :]
    carry_ref[j, :, tn:2 * tn] = uv_ref[tm:tm + SUBLANES, :]

    def conv(u_ref, cw_ref, cb_ref):
        acc = u_ref[pl.ds(SUBLANES - 2, tm), :] * cw_ref[0:1, :]
        acc = acc + u_ref[pl.ds(SUBLANES - 1, tm), :] * cw_ref[1:2, :]
        acc = acc + u_ref[pl.ds(SUBLANES, tm), :] * cw_ref[2:3, :]
        return acc + cb_ref[...]

    gate = conv(ug_ref, cwg_ref, cbg_ref)
    val = conv(uv_ref, cwv_ref, cbv_ref)
    silu = gate * (1.0 / (1.0 + jnp.exp(-gate)))
    o_ref[...] = (silu * val).astype(o_ref.dtype)


def _ffn_in(x, g, w_in, conv_w, conv_b, tm=512, tn=512):
    m, d = x.shape
    nf = w_in.shape[1] // 2
    nj = nf // tn
    assert m % tm == 0 and nf % tn == 0
    kern = functools.partial(_ffn_in_kernel, tm=tm, tn=tn)
    return pl.pallas_call(
        kern,
        grid=(m // tm, nj),
        in_specs=[
            pl.BlockSpec((tm, d), lambda i, j: (i, 0)),
            pl.BlockSpec((1, d), lambda i, j: (0, 0)),
            pl.BlockSpec((d, tn), lambda i, j: (0, j)),
            pl.BlockSpec((d, tn), lambda i, j: (0, j + nj)),
            pl.BlockSpec((CONV_WIDTH, tn), lambda i, j: (0, j)),
            pl.BlockSpec((CONV_WIDTH, tn), lambda i, j: (0, j + nj)),
            pl.BlockSpec((1, tn), lambda i, j: (0, j)),
            pl.BlockSpec((1, tn), lambda i, j: (0, j + nj)),
        ],
        out_specs=pl.BlockSpec((tm, tn), lambda i, j: (i, j)),
        out_shape=jax.ShapeDtypeStruct((m, nf), BF16),
        scratch_shapes=[
            pltpu.VMEM((tm, d), BF16),
            pltpu.VMEM((tm + SUBLANES, tn), F32),
            pltpu.VMEM((tm + SUBLANES, tn), F32),
            pltpu.VMEM((nj, SUBLANES, 2 * tn), F32),
        ],
        compiler_params=_params(2),
        name="ffn_in",
    )(x, g.reshape(1, d), w_in, w_in, conv_w, conv_w, conv_b.reshape(1, -1), conv_b.reshape(1, -1))


SB_TQ = 512
SB_TK = 256


def _sb_kernel(q_ref, k_ref, v_ref, o_ref, acc_ref, later_ref):
    i = pl.program_id(1)
    tq, tk = SB_TQ, SB_TK
    q = q_ref[...] * jnp.asarray(SB_HEAD_DIM ** -0.5, BF16)
    kk = lax.broadcasted_iota(jnp.int32, (tk, tk), 0)
    jj = lax.broadcasted_iota(jnp.int32, (tk, tk), 1)
    tri = (kk >= jj).astype(BF16)

    acc_ref[...] = jnp.zeros_like(acc_ref)
    later_ref[...] = jnp.zeros_like(later_ref)

    def block(j, masked):
        start = pl.multiple_of(j * tk, tk)
        kj = k_ref[pl.ds(start, tk), :]
        vj = v_ref[pl.ds(start, tk), :]
        z = lax.dot_general(q, kj, (((1,), (1,)), ((), ())), preferred_element_type=F32)
        log_not = jnp.minimum(z, 0.0) - jnp.log(1.0 + jnp.exp(-jnp.abs(z))) - z
        if masked:
            q_pos = i * tq + lax.broadcasted_iota(jnp.int32, (tq, tk), 0)
            k_pos = j * tk + lax.broadcasted_iota(jnp.int32, (tq, tk), 1)
            causal = k_pos < q_pos
            log_not = jnp.where(causal, log_not, 0.0)
        hi = log_not.astype(BF16)
        lo = (log_not - hi.astype(F32)).astype(BF16)
        within = (jnp.dot(hi, tri, preferred_element_type=F32)
                  + jnp.dot(lo, tri, preferred_element_type=F32))
        a = jnp.exp(z + within + later_ref[...])
        if masked:
            a = jnp.where(causal, a, 0.0)
        acc_ref[...] += jnp.dot(a.astype(BF16), vj, preferred_element_type=F32)
        later_ref[...] += within[:, 0:1]

    ratio = tq // tk
    for r in range(ratio - 1, -1, -1):
        block(i * ratio + r, masked=True)

    def body(it, carry):
        block(i * ratio - 1 - it, masked=False)
        return carry

    lax.fori_loop(0, i * ratio, body, 0)
    o_ref[...] = acc_ref[...].astype(o_ref.dtype)


def _sb_attention(qkv):
    s = qkv.shape[0]
    hd = SB_HEAD_DIM
    return pl.pallas_call(
        _sb_kernel,
        grid=(SB_HEADS, s // SB_TQ),
        in_specs=[
            pl.BlockSpec((SB_TQ, hd), lambda h, i: (i, h)),
            pl.BlockSpec((s, hd), lambda h, i: (0, SB_HEADS + h)),
            pl.BlockSpec((s, hd), lambda h, i: (0, 2 * SB_HEADS + h)),
        ],
        out_specs=pl.BlockSpec((SB_TQ, hd), lambda h, i: (i, h)),
        out_shape=jax.ShapeDtypeStruct((s, SB_WIDTH), BF16),
        scratch_shapes=[pltpu.VMEM((SB_TQ, hd), F32), pltpu.VMEM((SB_TQ, 1), F32)],
        compiler_params=_params(2),
        name="sb_attention",
    )(qkv, qkv, qkv)


def _sg_kernel(zg_ref, w_ref, lng_ref, lnb_ref, bias_ref, o_ref, *, tm):
    for c in range(tm // CHUNK):
        rows = slice(c * CHUNK, (c + 1) * CHUNK)
        for g in range(SG_GROUPS):
            cu = slice(g * SG_GROUP_DIM, (g + 1) * SG_GROUP_DIM)
            cv = slice(SG_WIDTH + g * SG_GROUP_DIM, SG_WIDTH + (g + 1) * SG_GROUP_DIM)
            v = zg_ref[rows, cv]
            mu = jnp.mean(v, axis=-1, keepdims=True)
            dv = v - mu
            var = jnp.mean(dv * dv, axis=-1, keepdims=True)
            vn = (dv * lax.rsqrt(var + EPS) * lng_ref[:, cu] + lnb_ref[:, cu]).astype(BF16)
            mixed = jnp.dot(w_ref[g], vn, preferred_element_type=F32) + bias_ref[:, cu]
            o_ref[rows, cu] = (zg_ref[rows, cu] * mixed).astype(o_ref.dtype)


def _sg_mlp(zg, w_masked, ln_g, ln_b, bias_full, tm=256):
    s = zg.shape[0]
    return pl.pallas_call(
        functools.partial(_sg_kernel, tm=tm),
        grid=(s // tm,),
        in_specs=[
            pl.BlockSpec((tm, 2 * SG_WIDTH), lambda i: (i, 0)),
            pl.BlockSpec((SG_GROUPS, CHUNK, CHUNK), lambda i: (0, 0, 0)),
            pl.BlockSpec((1, SG_WIDTH), lambda i: (0, 0)),
            pl.BlockSpec((1, SG_WIDTH), lambda i: (0, 0)),
            pl.BlockSpec((CHUNK, SG_WIDTH), lambda i: (0, 0)),
        ],
        out_specs=pl.BlockSpec((tm, SG_WIDTH), lambda i: (i, 0)),
        out_shape=jax.ShapeDtypeStruct((s, SG_WIDTH), BF16),
        compiler_params=_params(1),
        name="sg_mlp",
    )(zg, w_masked, ln_g, ln_b, bias_full)


def _swa_kernel(sink_ref, q_ref, kc_ref, kp_ref, vc_ref, vp_ref, o_ref):
    i = pl.program_id(0)
    hd = SWA_HEAD_DIM
    row = lax.broadcasted_iota(jnp.int32, (BLOCK, 2 * BLOCK), 0)
    col = lax.broadcasted_iota(jnp.int32, (BLOCK, 2 * BLOCK), 1)
    diff = row + BLOCK - col
    valid = (diff >= 0) & (diff < WINDOW) & ((col >= BLOCK) | (i > 0))
    for hk in range(SWA_KV_HEADS):
        ck = slice(hk * hd, (hk + 1) * hd)
        kcat = jnp.concatenate([kp_ref[:, ck], kc_ref[:, ck]], axis=0)
        vcat = jnp.concatenate([vp_ref[:, ck], vc_ref[:, ck]], axis=0)
        for g in range(SWA_GROUP):
            hq = hk * SWA_GROUP + g
            cq = slice(hq * hd, (hq + 1) * hd)
            s = lax.dot_general(q_ref[:, cq], kcat, (((1,), (1,)), ((), ())),
                                preferred_element_type=F32) * (hd ** -0.5)
            s = jnp.where(valid, s, -jnp.inf)
            sink = sink_ref[hq]
            m = jnp.maximum(jnp.max(s, axis=-1, keepdims=True), sink)
            p = jnp.exp(s - m)
            denom = jnp.sum(p, axis=-1, keepdims=True) + jnp.exp(sink - m)
            o = jnp.dot(p.astype(BF16), vcat, preferred_element_type=F32) / denom
            o_ref[:, cq] = o.astype(o_ref.dtype)


def _swa_attention(qkv, sinks):
    s = qkv.shape[0]
    kcol = SWA_Q_WIDTH // SWA_KV_WIDTH
    return pl.pallas_call(
        _swa_kernel,
        grid=(s // BLOCK,),
        in_specs=[
            pl.BlockSpec(memory_space=pltpu.SMEM),
            pl.BlockSpec((BLOCK, SWA_Q_WIDTH), lambda i: (i, 0)),
            pl.BlockSpec((BLOCK, SWA_KV_WIDTH), lambda i: (i, kcol)),
            pl.BlockSpec((BLOCK, SWA_KV_WIDTH), lambda i: (jnp.maximum(i - 1, 0), kcol)),
            pl.BlockSpec((BLOCK, SWA_KV_WIDTH), lambda i: (i, kcol + 1)),
            pl.BlockSpec((BLOCK, SWA_KV_WIDTH), lambda i: (jnp.maximum(i - 1, 0), kcol + 1)),
        ],
        out_specs=pl.BlockSpec((BLOCK, SWA_Q_WIDTH), lambda i: (i, 0)),
        out_shape=jax.ShapeDtypeStruct((s, SWA_Q_WIDTH), BF16),
        compiler_params=_params(1),
        name="swa_attention",
    )(sinks, qkv, qkv, qkv, qkv, qkv)


def _final_norm_kernel(x_ref, g_ref, o_ref):
    o_ref[...] = _rms_norm_rows(x_ref[...], g_ref[...])


def _final_norm(x, g, tm=512):
    m, d = x.shape
    return pl.pallas_call(
        _final_norm_kernel,
        grid=(m // tm,),
        in_specs=[pl.BlockSpec((tm, d), lambda i: (i, 0)), pl.BlockSpec((1, d), lambda i: (0, 0))],
        out_specs=pl.BlockSpec((tm, d), lambda i: (i, 0)),
        out_shape=jax.ShapeDtypeStruct((m, d), F32),
        compiler_params=_params(1),
        name="final_norm",
    )(x, g.reshape(1, d))


def kernel(x, mix_norm, ffn_norm, final_norm, ev_w_in, ev_sg_ln_g, ev_sg_ln_b, ev_sg_w, ev_sg_b,
           ev_w_out, od_w_qkv, od_b_qkv, od_sinks, od_w_out, od_b_out, ffn_w_in, ffn_conv_w,
           ffn_conv_b, ffn_w_down):
    b, s, d = x.shape
    assert (b, s, d) == (1, SEQ, D_MODEL)
    xs = x.reshape(s, d)
    tri = jnp.tril(jnp.ones((CHUNK, CHUNK), dtype=bool))
    for layer in range(DEPTH):
        if layer % 2 == 0:
            e = layer // 2
            w_in = ev_w_in[e].astype(BF16)
            qkv = _norm_mm(xs, mix_norm[layer], w_in[:, :3 * SB_WIDTH])
            zg = _norm_mm(xs, mix_norm[layer], w_in[:, 3 * SB_WIDTH:], act="gelu", out_dtype=F32)
            a_out = _sb_attention(qkv)
            w_masked = jnp.where(tri[None], ev_sg_w[e], 0.0).astype(BF16)
            bias_full = jnp.repeat(ev_sg_b[e].T, SG_GROUP_DIM, axis=1)
            b_out = _sg_mlp(zg, w_masked, ev_sg_ln_g[e].reshape(1, SG_WIDTH),
                            ev_sg_ln_b[e].reshape(1, SG_WIDTH), bias_full)
            mixed = jnp.concatenate([a_out, b_out], axis=1)
            xs = _mm_res(mixed, ev_w_out[e].astype(BF16), xs)
        else:
            o = layer // 2
            qkv = _norm_mm(xs, mix_norm[layer], od_w_qkv[o].astype(BF16), bias=od_b_qkv[o])
            att = _swa_attention(qkv, od_sinks[o])
            xs = _mm_res(att, od_w_out[o].astype(BF16), xs, bias=od_b_out[o])
        y = _ffn_in(xs, ffn_norm[layer], ffn_w_in[layer].astype(BF16), ffn_conv_w[layer],
                    ffn_conv_b[layer])
        xs = _mm_res(y, ffn_w_down[layer].astype(BF16), xs)
    return _final_norm(xs, final_norm).reshape(b, s, d)
```

```python
import functools

import jax
import jax.numpy as jnp
from jax import lax
from jax.experimental import pallas as pl
from jax.experimental.pallas import tpu as pltpu

F32 = jnp.float32
BF16 = jnp.bfloat16

D_MODEL = 4096
SEQ = 16384
DEPTH = 4
EPS = 1e-5
BLOCK = 128
SB_HEAD_DIM = 256
SB_HEADS = 8
SB_WIDTH = SB_HEADS * SB_HEAD_DIM
CHUNK = 128
SG_GROUP_DIM = 128
SG_GROUPS = 16
SG_WIDTH = SG_GROUPS * SG_GROUP_DIM
EV_OUT_COLS = SB_WIDTH + SG_WIDTH
SWA_HEAD_DIM = 64
SWA_HEADS = 64
SWA_KV_HEADS = 8
SWA_GROUP = SWA_HEADS // SWA_KV_HEADS
SWA_Q_WIDTH = SWA_HEADS * SWA_HEAD_DIM
SWA_KV_WIDTH = SWA_KV_HEADS * SWA_HEAD_DIM
WINDOW = 128
D_FF = 6144
CONV_WIDTH = 3

V7X_VMEM_LIMIT_BYTES = 56 * 1024 * 1024
SUBLANES = 8
LANES = 128


def _params(n_axes, flags=None):
    return pltpu.CompilerParams(
        dimension_semantics=("arbitrary",) * n_axes,
        vmem_limit_bytes=V7X_VMEM_LIMIT_BYTES,
        flags=flags,
    )


def _rms_norm_rows(x, g):
    ms = jnp.mean(x * x, axis=-1, keepdims=True)
    return x * lax.rsqrt(ms + EPS) * g


def _gelu_tanh(x):
    c = 0.7978845608028654
    return 0.5 * x * (1.0 + jnp.tanh(c * (x + 0.044715 * (x * x * x))))


def _dot_nt(a, b):
    return lax.dot_general(a, b, (((1,), (1,)), ((), ())), preferred_element_type=F32)


def _norm_mm_kernel(x_ref, g_ref, w_ref, *rest, has_bias, act):
    if has_bias:
        b_ref, o_ref, h_ref = rest
    else:
        o_ref, h_ref = rest

    @pl.when(pl.program_id(1) == 0)
    def _():
        h_ref[...] = _rms_norm_rows(x_ref[...], g_ref[...]).astype(BF16)

    acc = jnp.dot(h_ref[...], w_ref[...], preferred_element_type=F32)
    if has_bias:
        acc = acc + b_ref[...]
    if act == "gelu":
        acc = _gelu_tanh(acc)
    o_ref[...] = acc.astype(o_ref.dtype)


def _norm_mm(x, g, w, col0, n, bias=None, act=None, out_dtype=BF16, tm=512, tn=1024):
    m, d = x.shape
    assert m % tm == 0 and n % tn == 0 and col0 % tn == 0
    jb = col0 // tn
    in_specs = [
        pl.BlockSpec((tm, d), lambda i, j: (i, 0)),
        pl.BlockSpec((1, d), lambda i, j: (0, 0)),
        pl.BlockSpec((d, tn), lambda i, j: (0, j + jb)),
    ]
    args = [x, g.reshape(1, d), w]
    if bias is not None:
        in_specs.append(pl.BlockSpec((1, tn), lambda i, j: (0, j + jb)))
        args.append(bias.reshape(1, -1))
    return pl.pallas_call(
        functools.partial(_norm_mm_kernel, has_bias=bias is not None, act=act),
        grid=(m // tm, n // tn),
        in_specs=in_specs,
        out_specs=pl.BlockSpec((tm, tn), lambda i, j: (i, j)),
        out_shape=jax.ShapeDtypeStruct((m, n), out_dtype),
        scratch_shapes=[pltpu.VMEM((tm, d), BF16)],
        compiler_params=_params(2),
        name="norm_mm",
    )(*args)


def _mm_res_kernel(a_ref, w_ref, x_ref, *rest, has_bias):
    if has_bias:
        b_ref, o_ref = rest
    else:
        (o_ref,) = rest
    acc = jnp.dot(a_ref[...], w_ref[...], preferred_element_type=F32)
    if has_bias:
        acc = acc + b_ref[...]
    o_ref[...] = x_ref[...] + acc


def _mm_res(a, w, x, bias=None, tm=1024, tn=512):
    m, k = a.shape
    n = w.shape[1]
    assert m % tm == 0 and n % tn == 0
    in_specs = [
        pl.BlockSpec((tm, k), lambda i, j: (i, 0)),
        pl.BlockSpec((k, tn), lambda i, j: (0, j)),
        pl.BlockSpec((tm, tn), lambda i, j: (i, j)),
    ]
    args = [a, w, x]
    if bias is not None:
        in_specs.append(pl.BlockSpec((1, tn), lambda i, j: (0, j)))
        args.append(bias.reshape(1, n))
    return pl.pallas_call(
        functools.partial(_mm_res_kernel, has_bias=bias is not None),
        grid=(m // tm, n // tn),
        in_specs=in_specs,
        out_specs=pl.BlockSpec((tm, tn), lambda i, j: (i, j)),
        out_shape=jax.ShapeDtypeStruct((m, n), F32),
        compiler_params=_params(2),
        name="mm_res",
    )(*args)


def _ffn_in_kernel(x_ref, g_ref, wg_ref, wv_ref, cwg_ref, cwv_ref, cbg_ref, cbv_ref,
                   o_ref, h_ref, ug_ref, uv_ref, carry_ref, *, tm, tn):
    i = pl.program_id(0)
    j = pl.program_id(1)

    @pl.when(j == 0)
    def _():
        h_ref[...] = _rms_norm_rows(x_ref[...], g_ref[...]).astype(BF16)

    @pl.when(i == 0)
    def _():
        ug_ref[0:SUBLANES, :] = jnp.zeros((SUBLANES, tn), F32)
        uv_ref[0:SUBLANES, :] = jnp.zeros((SUBLANES, tn), F32)

    @pl.when(i > 0)
    def _():
        ug_ref[0:SUBLANES, :] = carry_ref[j, :, 0:tn]
        uv_ref[0:SUBLANES, :] = carry_ref[j, :, tn:2 * tn]

    h = h_ref[...]
    ug_ref[SUBLANES:SUBLANES + tm, :] = jnp.dot(h, wg_ref[...], preferred_element_type=F32)
    uv_ref[SUBLANES:SUBLANES + tm, :] = jnp.dot(h, wv_ref[...], preferred_element_type=F32)
    carry_ref[j, :, 0:tn] = ug_ref[tm:tm + SUBLANES, :]
    carry_ref[j, :, tn:2 * tn] = uv_ref[tm:tm + SUBLANES, :]

    def conv(u_ref, cw_ref, cb_ref):
        acc = u_ref[pl.ds(SUBLANES - 2, tm), :] * cw_ref[0:1, :]
        acc = acc + u_ref[pl.ds(SUBLANES - 1, tm), :] * cw_ref[1:2, :]
        acc = acc + u_ref[pl.ds(SUBLANES, tm), :] * cw_ref[2:3, :]
        return acc + cb_ref[...]

    gate = conv(ug_ref, cwg_ref, cbg_ref)
    val = conv(uv_ref, cwv_ref, cbv_ref)
    silu = gate * (1.0 / (1.0 + jnp.exp(-gate)))
    o_ref[...] = (silu * val).astype(o_ref.dtype)


def _ffn_in(x, g, w_in, conv_w, conv_b, tm=512, tn=512):
    m, d = x.shape
    nf = w_in.shape[1] // 2
    nj = nf // tn
    assert m % tm == 0 and nf % tn == 0
    kern = functools.partial(_ffn_in_kernel, tm=tm, tn=tn)
    return pl.pallas_call(
        kern,
        grid=(m // tm, nj),
        in_specs=[
            pl.BlockSpec((tm, d), lambda i, j: (i, 0)),
            pl.BlockSpec((1, d), lambda i, j: (0, 0)),
            pl.BlockSpec((d, tn), lambda i, j: (0, j)),
            pl.BlockSpec((d, tn), lambda i, j: (0, j + nj)),
            pl.BlockSpec((CONV_WIDTH, tn), lambda i, j: (0, j)),
            pl.BlockSpec((CONV_WIDTH, tn), lambda i, j: (0, j + nj)),
            pl.BlockSpec((1, tn), lambda i, j: (0, j)),
            pl.BlockSpec((1, tn), lambda i, j: (0, j + nj)),
        ],
        out_specs=pl.BlockSpec((tm, tn), lambda i, j: (i, j)),
        out_shape=jax.ShapeDtypeStruct((m, nf), BF16),
        scratch_shapes=[
            pltpu.VMEM((tm, d), BF16),
            pltpu.VMEM((tm + SUBLANES, tn), F32),
            pltpu.VMEM((tm + SUBLANES, tn), F32),
            pltpu.VMEM((nj, SUBLANES, 2 * tn), F32),
        ],
        compiler_params=_params(2),
        name="ffn_in",
    )(x, g.reshape(1, d), w_in, w_in, conv_w, conv_w, conv_b.reshape(1, -1), conv_b.reshape(1, -1))


SB_TQ = 512
SB_TK = 256
SB_Z_SLOTS = 4
SB_MASKED = -1e30


def _sb_kernel(q_ref, k_ref, v_ref, o_ref, acc_ref, later_ref, z_ref, hi_ref, lo_ref, w_ref,
               a_ref):
    i = pl.program_id(1)
    tq, tk = SB_TQ, SB_TK
    assert tq == 2 * tk
    n_blocks = 2 * (i + 1)
    q = q_ref[...] * jnp.asarray(SB_HEAD_DIM ** -0.5, BF16)
    kk = lax.broadcasted_iota(jnp.int32, (tk, tk), 0)
    jj = lax.broadcasted_iota(jnp.int32, (tk, tk), 1)
    tri = (kk >= jj).astype(BF16)

    acc_ref[...] = jnp.zeros_like(acc_ref)
    later_ref[...] = jnp.zeros_like(later_ref)

    def key_rows(m):
        return pl.ds(pl.multiple_of((n_blocks - 1 - m) * tk, tk), tk)

    def scores(m, masked):
        z = _dot_nt(q, k_ref[key_rows(m), :])
        if masked:
            col_minus_row = (lax.broadcasted_iota(jnp.int32, (tq, tk), 1)
                             - lax.broadcasted_iota(jnp.int32, (tq, tk), 0))
            z = jnp.where(col_minus_row < i * tq - (n_blocks - 1 - m) * tk, z, SB_MASKED)
        z_ref[m % SB_Z_SLOTS] = z

    def softplus_split(m):
        z = z_ref[m % SB_Z_SLOTS]
        sp = jnp.maximum(z, 0.0) + jnp.log(1.0 + jnp.exp(-jnp.abs(z)))
        hi = sp.astype(BF16)
        hi_ref[m % 2] = hi
        lo_ref[m % 2] = (sp - hi.astype(F32)).astype(BF16)

    def suffix(m):
        w_ref[m % 2] = (jnp.dot(hi_ref[m % 2], tri, preferred_element_type=F32)
                        + jnp.dot(lo_ref[m % 2], tri, preferred_element_type=F32))

    def weights(m):
        within = w_ref[m % 2]
        later = later_ref[...]
        a_ref[m % 2] = jnp.exp(z_ref[m % SB_Z_SLOTS] - within - later).astype(BF16)
        later_ref[...] = later + within[:, 0:1]

    def accumulate(m):
        acc_ref[...] += jnp.dot(a_ref[m % 2], v_ref[key_rows(m), :], preferred_element_type=F32)

    stages = (scores, softplus_split, suffix, weights, accumulate)

    def step(t, first, last):
        for s in range(last, first - 1, -1):
            if s == 0:
                scores(t, masked=isinstance(t, int) and t < 2)
            else:
                stages[s](t - s)

    @pl.when(i == 0)
    def _():
        for m in range(2):
            scores(m, masked=True)
            softplus_split(m)
            suffix(m)
            weights(m)
            accumulate(m)

    @pl.when(i > 0)
    def _():
        n_stages = len(stages)
        for t in range(n_stages - 1):
            step(t, 0, t)

        def body(p, carry):
            t = n_stages - 1 + 2 * p
            step(t, 0, n_stages - 1)
            step(t + 1, 0, n_stages - 1)
            return carry

        lax.fori_loop(0, (n_blocks - (n_stages - 1)) // 2, body, 0)
        for d in range(1, n_stages):
            step(n_blocks - 1 + d, d, n_stages - 1)

    o_ref[...] = acc_ref[...].astype(o_ref.dtype)


def _sb_attention(qkv):
    s = qkv.shape[0]
    hd = SB_HEAD_DIM
    return pl.pallas_call(
        _sb_kernel,
        grid=(SB_HEADS, s // SB_TQ),
        in_specs=[
            pl.BlockSpec((SB_TQ, hd), lambda h, i: (i, h)),
            pl.BlockSpec((s, hd), lambda h, i: (0, SB_HEADS + h)),
            pl.BlockSpec((s, hd), lambda h, i: (0, 2 * SB_HEADS + h)),
        ],
        out_specs=pl.BlockSpec((SB_TQ, hd), lambda h, i: (i, h)),
        out_shape=jax.ShapeDtypeStruct((s, EV_OUT_COLS), BF16),
        scratch_shapes=[
            pltpu.VMEM((SB_TQ, hd), F32),
            pltpu.VMEM((SB_TQ, 1), F32),
            pltpu.VMEM((SB_Z_SLOTS, SB_TQ, SB_TK), F32),
            pltpu.VMEM((2, SB_TQ, SB_TK), BF16),
            pltpu.VMEM((2, SB_TQ, SB_TK), BF16),
            pltpu.VMEM((2, SB_TQ, SB_TK), F32),
            pltpu.VMEM((2, SB_TQ, SB_TK), BF16),
        ],
        compiler_params=_params(2),
        name="sb_attention",
    )(qkv, qkv, qkv)


def _sg_kernel(zg_ref, w_ref, lng_ref, lnb_ref, bias_ref, mix_ref, o_ref, *, tm):
    del mix_ref
    for c in range(tm // CHUNK):
        rows = slice(c * CHUNK, (c + 1) * CHUNK)
        for g in range(SG_GROUPS):
            cu = slice(g * SG_GROUP_DIM, (g + 1) * SG_GROUP_DIM)
            cv = slice(SG_WIDTH + g * SG_GROUP_DIM, SG_WIDTH + (g + 1) * SG_GROUP_DIM)
            v = zg_ref[rows, cv]
            mu = jnp.mean(v, axis=-1, keepdims=True)
            dv = v - mu
            var = jnp.mean(dv * dv, axis=-1, keepdims=True)
            vn = (dv * lax.rsqrt(var + EPS) * lng_ref[:, cu] + lnb_ref[:, cu]).astype(BF16)
            mixed = jnp.dot(w_ref[g], vn, preferred_element_type=F32) + bias_ref[:, cu]
            o_ref[rows, cu] = (zg_ref[rows, cu] * mixed).astype(o_ref.dtype)


def _sg_mlp(zg, w_masked, ln_g, ln_b, bias_full, mix, tm=256):
    s = zg.shape[0]
    return pl.pallas_call(
        functools.partial(_sg_kernel, tm=tm),
        grid=(s // tm,),
        in_specs=[
            pl.BlockSpec((tm, 2 * SG_WIDTH), lambda i: (i, 0)),
            pl.BlockSpec((SG_GROUPS, CHUNK, CHUNK), lambda i: (0, 0, 0)),
            pl.BlockSpec((1, SG_WIDTH), lambda i: (0, 0)),
            pl.BlockSpec((1, SG_WIDTH), lambda i: (0, 0)),
            pl.BlockSpec((CHUNK, SG_WIDTH), lambda i: (0, 0)),
            pl.BlockSpec(memory_space=pl.ANY),
        ],
        out_specs=pl.BlockSpec((tm, SG_WIDTH), lambda i: (i, SB_WIDTH // SG_WIDTH)),
        out_shape=jax.ShapeDtypeStruct(mix.shape, mix.dtype),
        input_output_aliases={5: 0},
        compiler_params=_params(1),
        name="sg_mlp",
    )(zg, w_masked, ln_g, ln_b, bias_full, mix)


def _swa_kernel(sink_ref, q_ref, kc_ref, kp_ref, vc_ref, vp_ref, o_ref):
    i = pl.program_id(0)
    hd = SWA_HEAD_DIM
    pairs = SWA_GROUP // 2
    row = lax.broadcasted_iota(jnp.int32, (BLOCK, 2 * BLOCK), 0)
    col = lax.broadcasted_iota(jnp.int32, (BLOCK, 2 * BLOCK), 1)
    diff = row + BLOCK - col
    valid = (diff >= 0) & (diff < WINDOW) & ((col >= BLOCK) | (i > 0))
    bias1 = jnp.where(valid, 0.0, -jnp.inf).astype(F32)
    bias = jnp.concatenate([bias1] * SWA_GROUP, axis=0)
    left = lax.broadcasted_iota(jnp.int32, (1, LANES), 1) < hd

    def both_halves(prev_ref, cur_ref, hk):
        pa = slice((hk // 2) * LANES, (hk // 2 + 1) * LANES)
        pair = jnp.concatenate([prev_ref[:, pa], cur_ref[:, pa]], axis=0).astype(F32)
        swapped = pltpu.roll(pair, hd, axis=1)
        own_left = (hk % 2) == 0
        return jnp.where(left == own_left, pair, swapped).astype(BF16)

    for hk in range(SWA_KV_HEADS):
        k2 = both_halves(kp_ref, kc_ref, hk)
        v2 = both_halves(vp_ref, vc_ref, hk)
        h0 = hk * SWA_GROUP
        qb = jnp.concatenate(
            [q_ref[:, (h0 + 2 * p) * hd:(h0 + 2 * p + 2) * hd] for p in range(pairs)], axis=0)
        qb = qb * jnp.asarray(hd ** -0.5, BF16)
        zero = jnp.zeros_like(qb)
        lhs = jnp.concatenate([jnp.where(left, qb, zero), jnp.where(left, zero, qb)], axis=0)
        s = _dot_nt(lhs, k2) + bias
        sink = jnp.concatenate(
            [jnp.full((BLOCK, LANES), sink_ref[h0 + 2 * p + par], F32)
             for par in range(2) for p in range(pairs)], axis=0)
        row_max = jnp.broadcast_to(jnp.max(s, axis=-1, keepdims=True), sink.shape)
        m = jnp.maximum(row_max, sink)
        p_un = jnp.exp(s - jnp.concatenate([m, m], axis=1)).astype(BF16)
        ov = jnp.dot(p_un, jnp.concatenate([v2, jnp.ones_like(v2)], axis=1),
                     preferred_element_type=F32)
        denom = ov[:, LANES:2 * LANES] + jnp.exp(sink - m)
        o2 = ov[:, 0:LANES] / denom
        half = pairs * BLOCK
        o = jnp.where(left, o2[0:half], o2[half:2 * half]).astype(o_ref.dtype)
        for p in range(pairs):
            o_ref[:, (h0 + 2 * p) * hd:(h0 + 2 * p + 2) * hd] = o[p * BLOCK:(p + 1) * BLOCK]


def _swa_attention(qkv, sinks):
    s = qkv.shape[0]
    kcol = SWA_Q_WIDTH // SWA_KV_WIDTH
    return pl.pallas_call(
        _swa_kernel,
        grid=(s // BLOCK,),
        in_specs=[
            pl.BlockSpec(memory_space=pltpu.SMEM),
            pl.BlockSpec((BLOCK, SWA_Q_WIDTH), lambda i: (i, 0)),
            pl.BlockSpec((BLOCK, SWA_KV_WIDTH), lambda i: (i, kcol)),
            pl.BlockSpec((BLOCK, SWA_KV_WIDTH), lambda i: (jnp.maximum(i - 1, 0), kcol)),
            pl.BlockSpec((BLOCK, SWA_KV_WIDTH), lambda i: (i, kcol + 1)),
            pl.BlockSpec((BLOCK, SWA_KV_WIDTH), lambda i: (jnp.maximum(i - 1, 0), kcol + 1)),
        ],
        out_specs=pl.BlockSpec((BLOCK, SWA_Q_WIDTH), lambda i: (i, 0)),
        out_shape=jax.ShapeDtypeStruct((s, SWA_Q_WIDTH), BF16),
        compiler_params=_params(1),
        name="swa_attention",
    )(sinks, qkv, qkv, qkv, qkv, qkv)


def _final_norm_kernel(x_ref, g_ref, o_ref):
    o_ref[...] = _rms_norm_rows(x_ref[...], g_ref[...])


def _final_norm(x, g, tm=512):
    m, d = x.shape
    return pl.pallas_call(
        _final_norm_kernel,
        grid=(m // tm,),
        in_specs=[pl.BlockSpec((tm, d), lambda i: (i, 0)), pl.BlockSpec((1, d), lambda i: (0, 0))],
        out_specs=pl.BlockSpec((tm, d), lambda i: (i, 0)),
        out_shape=jax.ShapeDtypeStruct((m, d), F32),
        compiler_params=_params(1),
        name="final_norm",
    )(x, g.reshape(1, d))


def kernel(x, mix_norm, ffn_norm, final_norm, ev_w_in, ev_sg_ln_g, ev_sg_ln_b, ev_sg_w, ev_sg_b,
           ev_w_out, od_w_qkv, od_b_qkv, od_sinks, od_w_out, od_b_out, ffn_w_in, ffn_conv_w,
           ffn_conv_b, ffn_w_down):
    b, s, d = x.shape
    assert (b, s, d) == (1, SEQ, D_MODEL)
    xs = x.reshape(s, d)
    tri = jnp.tril(jnp.ones((CHUNK, CHUNK), dtype=bool))
    for layer in range(DEPTH):
        if layer % 2 == 0:
            e = layer // 2
            w_in = ev_w_in[e].astype(BF16)
            qkv = _norm_mm(xs, mix_norm[layer], w_in, 0, 3 * SB_WIDTH)
            zg = _norm_mm(xs, mix_norm[layer], w_in, 3 * SB_WIDTH, 2 * SG_WIDTH, act="gelu",
                          out_dtype=F32)
            mix = _sb_attention(qkv)
            w_masked = jnp.where(tri[None], ev_sg_w[e], 0.0).astype(BF16)
            bias_full = jnp.repeat(ev_sg_b[e].T, SG_GROUP_DIM, axis=1)
            mix = _sg_mlp(zg, w_masked, ev_sg_ln_g[e].reshape(1, SG_WIDTH),
                          ev_sg_ln_b[e].reshape(1, SG_WIDTH), bias_full, mix)
            xs = _mm_res(mix, ev_w_out[e].astype(BF16), xs)
        else:
            o = layer // 2
            qkv = _norm_mm(xs, mix_norm[layer], od_w_qkv[o].astype(BF16), 0,
                           SWA_Q_WIDTH + 2 * SWA_KV_WIDTH, bias=od_b_qkv[o])
            att = _swa_attention(qkv, od_sinks[o])
            xs = _mm_res(att, od_w_out[o].astype(BF16), xs, bias=od_b_out[o])
        y = _ffn_in(xs, ffn_norm[layer], ffn_w_in[layer].astype(BF16), ffn_conv_w[layer],
                    ffn_conv_b[layer])
        xs = _mm_res(y, ffn_w_down[layer].astype(BF16), xs)
    return _final_norm(xs, final_norm).reshape(b, s, d)
```
